```python
import math
import jax, jax.numpy as jnp
from jax import lax
import numpy as np

D_MODEL = 1024
BATCH = 4
SEQ = 4096
DEPTH = 4

N_MIXERS = 2
N_GLA = (DEPTH + 1) // 2
N_POOL = DEPTH // 2

GLA_HEADS = 4
GLA_KD = D_MODEL // 2
GLA_VD = D_MODEL
GLA_DK = GLA_KD // GLA_HEADS
GLA_DV = GLA_VD // GLA_HEADS
GLA_GATE_RANK = 16
GLA_GATE_NORMALIZER = 16.0
GLA_CHUNK = 64
GLA_IN = 2 * GLA_KD + 2 * GLA_VD + GLA_GATE_RANK

POOL_EXPAND = 2
POOL_WIDTH = POOL_EXPAND * D_MODEL
POOL_WINDOWS = (2, 4, 8, 16)
POOL_GROUPS = len(POOL_WINDOWS)
POOL_GW = POOL_WIDTH // POOL_GROUPS

DEEPNORM_ALPHA = (2.0 * DEPTH) ** 0.25
DEEPNORM_BETA = (8.0 * DEPTH) ** -0.25
LN_EPS = 1e-5
RMS_EPS = 1e-6

kernel_name = "gla_pool_interleaved_deepnorm"


def _layernorm(x, g, b):
    xf = x.astype(jnp.float32)
    mu = jnp.mean(xf, axis=-1, keepdims=True)
    var = jnp.mean(jnp.square(xf - mu), axis=-1, keepdims=True)
    y = (xf - mu) * lax.rsqrt(var + LN_EPS) * g.astype(jnp.float32) + b.astype(jnp.float32)
    return y.astype(x.dtype)


def _gla_mixer(x, w_in, w_gk, b_gk, norm_w, w_out):
    B, T, _ = x.shape
    nc = T // GLA_CHUNK
    h = (x @ w_in).astype(jnp.float32)
    q, k, v, g, lr = jnp.split(
        h, [GLA_KD, 2 * GLA_KD, 2 * GLA_KD + GLA_VD, 2 * GLA_KD + 2 * GLA_VD], axis=-1)
    gk = jax.nn.log_sigmoid(lr @ w_gk.astype(jnp.float32) + b_gk.astype(jnp.float32))
    gk = gk / GLA_GATE_NORMALIZER

    def heads(t, dh):
        return t.reshape(B, nc, GLA_CHUNK, GLA_HEADS, dh).transpose(0, 3, 1, 2, 4)

    q = heads(q, GLA_DK) * (GLA_DK ** -0.5)
    k = heads(k, GLA_DK)
    v = heads(v, GLA_DV)
    gk = heads(gk, GLA_DK)

    bcum = jnp.cumsum(gk, axis=-2)
    b_last = bcum[..., -1:, :]
    qe = q * jnp.exp(bcum)
    ke = k * jnp.exp(-bcum)
    kd = k * jnp.exp(b_last - bcum)

    mask = jnp.tril(jnp.ones((GLA_CHUNK, GLA_CHUNK), dtype=bool))
    scores = jnp.einsum('bhncd,bhnsd->bhncs', qe, ke)
    scores = jnp.where(mask, scores, 0.0)
    o_intra = jnp.einsum('bhncs,bhnsv->bhncv', scores, v)

    kv = jnp.einsum('bhncd,bhncv->bhndv', kd, v)
    decay = jnp.exp(b_last[..., 0, :])

    def step(S, inp):
        dec, kvc = inp
        return dec[..., None] * S + kvc, S

    S0 = jnp.zeros((B, GLA_HEADS, GLA_DK, GLA_DV), jnp.float32)
    _, S_prev = lax.scan(step, S0, (jnp.moveaxis(decay, 2, 0), jnp.moveaxis(kv, 2, 0)))
    S_prev = jnp.moveaxis(S_prev, 0, 2)
    o = o_intra + jnp.einsum('bhncd,bhndv->bhncv', qe, S_prev)

    o = o.transpose(0, 2, 3, 1, 4).reshape(B, T, GLA_HEADS, GLA_DV)
    o = o * lax.rsqrt(jnp.mean(jnp.square(o), axis=-1, keepdims=True) + RMS_EPS)
    o = o * norm_w.astype(jnp.float32)
    o = o.reshape(B, T, GLA_VD) * jax.nn.silu(g)
    return o.astype(x.dtype) @ w_out


def _pool_mixer(x, w_in, w_grp, b_grp, scale, w_out):
    B, T, _ = x.shape
    h = x @ w_in
    u, z = jnp.split(h, [POOL_WIDTH], axis=-1)
    u = u.astype(jnp.float32).reshape(B, T, POOL_GROUPS, POOL_GW)
    cs = jnp.cumsum(u, axis=1)
    pos = jnp.arange(1, T + 1, dtype=jnp.float32)
    pooled = []
    for gi, w in enumerate(POOL_WINDOWS):
        cs_g = cs[:, :, gi]
        lag = jnp.pad(cs_g, ((0, 0), (w, 0), (0, 0)))[:, :T]
        cnt = jnp.minimum(pos, float(w))[None, :, None]
        pooled.append((cs_g - lag) / cnt)
    p = jnp.stack(pooled, axis=2) - u
    m = jnp.einsum('btgc,gcd->btgd', p, w_grp.astype(jnp.float32)) + b_grp.astype(jnp.float32)
    m = m.reshape(B, T, POOL_WIDTH) * scale.astype(jnp.float32)
    y = m * jax.nn.silu(z.astype(jnp.float32))
    return y.astype(x.dtype) @ w_out


def setup_inputs(seed: int = 0) -> dict:
    key = jax.random.key(seed)
    ks = jax.random.split(key, 16)
    f32 = jnp.float32
    nrm = lambda k, s, sc: (jax.random.normal(k, s, f32) * sc)
    return {
        "x": jax.random.normal(ks[0], (BATCH, SEQ, D_MODEL), f32),
        "gla_w_in": nrm(ks[1], (N_GLA, D_MODEL, GLA_IN), D_MODEL ** -0.5),
        "gla_w_gk": nrm(ks[2], (N_GLA, GLA_GATE_RANK, GLA_KD), GLA_GATE_RANK ** -0.5),
        "gla_b_gk": nrm(ks[3], (N_GLA, GLA_KD), 0.02),
        "gla_norm_w": 1.0 + nrm(ks[4], (N_GLA, GLA_DV), 0.02),
        "gla_w_out": nrm(ks[5], (N_GLA, GLA_VD, D_MODEL), GLA_VD ** -0.5 * DEEPNORM_BETA),
        "pool_w_in": nrm(ks[6], (N_POOL, D_MODEL, 2 * POOL_WIDTH), D_MODEL ** -0.5),
        "pool_w_grp": nrm(ks[7], (N_POOL, POOL_GROUPS, POOL_GW, POOL_GW), POOL_GW ** -0.5),
        "pool_b_grp": nrm(ks[8], (N_POOL, POOL_GROUPS, POOL_GW), 0.02),
        "pool_scale": 1.0 + nrm(ks[9], (N_POOL, POOL_WIDTH), 0.02),
        "pool_w_out": nrm(ks[10], (N_POOL, POOL_WIDTH, D_MODEL), POOL_WIDTH ** -0.5 * DEEPNORM_BETA),
        "ln_g": 1.0 + nrm(ks[11], (DEPTH, D_MODEL), 0.02),
        "ln_b": nrm(ks[12], (DEPTH, D_MODEL), 0.02),
    }


def reference(x, gla_w_in, gla_w_gk, gla_b_gk, gla_norm_w, gla_w_out,
              pool_w_in, pool_w_grp, pool_b_grp, pool_scale, pool_w_out, ln_g, ln_b):
    for i in range(DEPTH):
        j = i // N_MIXERS
        if i % N_MIXERS == 0:
            y = _gla_mixer(x, gla_w_in[j], gla_w_gk[j], gla_b_gk[j], gla_norm_w[j], gla_w_out[j])
        else:
            y = _pool_mixer(x, pool_w_in[j], pool_w_grp[j], pool_b_grp[j], pool_scale[j], pool_w_out[j])
        x = _layernorm(DEEPNORM_ALPHA * x + y, ln_g[i], ln_b[i])
    return x
```

```python
import functools

import jax
import jax.numpy as jnp
from jax import lax
from jax.experimental import pallas as pl
from jax.experimental.pallas import tpu as pltpu

D_MODEL = 1024
DEPTH = 4

GLA_HEADS = 4
GLA_KD = D_MODEL // 2
GLA_VD = D_MODEL
GLA_DK = GLA_KD // GLA_HEADS
GLA_DV = GLA_VD // GLA_HEADS
GLA_GATE_RANK = 16
GLA_GATE_NORMALIZER = 16.0
GLA_BLOCK = 128

POOL_WIDTH = 2 * D_MODEL
POOL_WINDOWS = (2, 4, 8, 16)
POOL_GROUPS = len(POOL_WINDOWS)
POOL_GW = POOL_WIDTH // POOL_GROUPS
POOL_HALO = 16

DEEPNORM_ALPHA = (2.0 * DEPTH) ** 0.25
LN_EPS = 1e-5
RMS_EPS = 1e-6

LANES = 128
TILE_M = 512
VMEM_LIMIT_BYTES = 56 * 1024 * 1024

_F32 = jnp.float32
_BF16 = jnp.bfloat16


def _dot(a, b):
    return jnp.dot(a, b, preferred_element_type=_F32)


def _dot_nt(a, b):
    return lax.dot_general(a, b, (((1,), (1,)), ((), ())), preferred_element_type=_F32)


def _silu(z):
    return z * (1.0 / (1.0 + jnp.exp(-z)))


def _residual_layernorm(x, y, g, b):
    r = DEEPNORM_ALPHA * x + y
    mu = jnp.mean(r, axis=-1, keepdims=True)
    d = r - mu
    var = jnp.mean(d * d, axis=-1, keepdims=True)
    return d * lax.rsqrt(var + LN_EPS) * g + b


def _gla_kernel(x_ref, wqkg_ref, wvt_ref, wlr_ref, wgk_ref, bgk_ref, nw_ref, wout_ref,
                lng_ref, lnb_ref, o_ref, q_s, k_s, sg_s, vt_s, bcum_s, y_s, st_s):
    tm = x_ref.shape[0]

    @pl.when(pl.program_id(1) == 0)
    def _():
        st_s[...] = jnp.zeros_like(st_s)

    x = x_ref[...]
    xb = x.astype(_BF16)

    q_s[...] = _dot(xb, wqkg_ref[:, 0:GLA_KD]) * (GLA_DK ** -0.5)
    k_s[...] = _dot(xb, wqkg_ref[:, GLA_KD:2 * GLA_KD])
    for j in range(GLA_VD // 512):
        g = _dot(xb, wqkg_ref[:, 2 * GLA_KD + 512 * j:2 * GLA_KD + 512 * (j + 1)])
        sg_s[:, 512 * j:512 * (j + 1)] = _silu(g)
    vt_s[...] = _dot_nt(wvt_ref[...], xb).astype(_BF16)

    lr = _dot(xb, wlr_ref[...])
    pre = _dot(lr.astype(_BF16), wgk_ref[...]) + bgk_ref[...]
    gk = jax.nn.log_sigmoid(pre) / GLA_GATE_NORMALIZER
    gk_hi = gk.astype(_BF16)
    gk_lo = (gk - gk_hi.astype(_F32)).astype(_BF16)
    row = lax.broadcasted_iota(jnp.int32, (GLA_BLOCK, GLA_BLOCK), 0)
    col = lax.broadcasted_iota(jnp.int32, (GLA_BLOCK, GLA_BLOCK), 1)
    causal = row >= col
    tri = causal.astype(_BF16)
    for p in range(tm // GLA_BLOCK):
        rows = slice(p * GLA_BLOCK, (p + 1) * GLA_BLOCK)
        bcum_s[rows, :] = _dot(tri, gk_hi[rows, :]) + _dot(tri, gk_lo[rows, :])

    for p in range(tm // GLA_BLOCK):
        r0 = p * GLA_BLOCK
        rows = slice(r0, r0 + GLA_BLOCK)
        b = bcum_s[rows, :]
        b_mid = bcum_s[r0 + GLA_BLOCK // 2 - 1:r0 + GLA_BLOCK // 2, :]
        b_last = bcum_s[r0 + GLA_BLOCK - 1:r0 + GLA_BLOCK, :]
        q = q_s[rows, :]
        k = k_s[rows, :]
        q_mid = (q * jnp.exp(b - b_mid)).astype(_BF16)
        k_mid = (k * jnp.exp(b_mid - b)).astype(_BF16)
        q_dec = (q * jnp.exp(b)).astype(_BF16)
        k_dec = (k * jnp.exp(b_last - b)).astype(_BF16)
        decay = jnp.exp(b_last)
        for h in range(GLA_HEADS):
            hs = slice(h * GLA_DK, (h + 1) * GLA_DK)
            vs = slice(h * GLA_DV, (h + 1) * GLA_DV)
            scores = _dot_nt(q_mid[:, hs], k_mid[:, hs])
            scores = jnp.where(causal, scores, 0.0).astype(_BF16)
            v_t = vt_s[vs, rows]
            state_t = st_s[h]
            lhs = jnp.concatenate([scores, q_dec[:, hs]], axis=1)
            rhs = jnp.concatenate([v_t, state_t.astype(_BF16)], axis=1)
            o = _dot_nt(lhs, rhs)
            st_s[h] = state_t * decay[:, hs] + _dot(v_t, k_dec[:, hs])
            ms = jnp.mean(o * o, axis=-1, keepdims=True)
            o = o * lax.rsqrt(ms + RMS_EPS) * nw_ref[...]
            y_s[rows, vs] = (o * sg_s[rows, vs]).astype(_BF16)

    out = _dot(y_s[...], wout_ref[...])
    o_ref[...] = _residual_layernorm(x, out, lng_ref[...], lnb_ref[...])


def _pool_kernel(x_ref, win_ref, wgrp_ref, bgrp_ref, scale_ref, wout_ref, lng_ref, lnb_ref,
                 o_ref, halo_s, y_s):
    tm = x_ref.shape[0]
    i = pl.program_id(1)

    @pl.when(i == 0)
    def _():
        halo_s[...] = jnp.zeros_like(halo_s)

    x = x_ref[...]
    xb = x.astype(_BF16)
    pos = (i * tm + lax.broadcasted_iota(jnp.int32, (tm, 1), 0) + 1).astype(_F32)

    for g, w in enumerate(POOL_WINDOWS):
        cs = slice(g * POOL_GW, (g + 1) * POOL_GW)
        u = _dot(xb, win_ref[:, cs])
        ext = jnp.concatenate([halo_s[:, cs], u], axis=0)
        halo_s[:, cs] = u[tm - POOL_HALO:, :]
        s = ext
        for lvl in range(g + 1):
            s = s + pltpu.roll(s, 2 ** lvl, axis=0)
        inv_cnt = 1.0 / jnp.minimum(pos, float(w))
        pooled = s[POOL_HALO:, :] * inv_cnt - u
        m = _dot(pooled.astype(_BF16), wgrp_ref[g]) + bgrp_ref[:, cs]
        m = m * scale_ref[:, cs]
        z = _dot(xb, win_ref[:, POOL_WIDTH + g * POOL_GW:POOL_WIDTH + (g + 1) * POOL_GW])
        y_s[:, cs] = (m * _silu(z)).astype(_BF16)

    out = _dot(y_s[...], wout_ref[...])
    o_ref[...] = _residual_layernorm(x, out, lng_ref[...], lnb_ref[...])


def _resident():
    return pl.BlockSpec(memory_space=pltpu.VMEM)


def _token_spec(tm):
    return pl.BlockSpec((None, tm, D_MODEL), lambda b, i: (b, i, 0))


def _compiler_params():
    return pltpu.CompilerParams(
        dimension_semantics=("arbitrary", "arbitrary"),
        vmem_limit_bytes=VMEM_LIMIT_BYTES,
    )


def _gla_layer(x, w_in, w_gk, b_gk, norm_w, w_out, ln_g, ln_b):
    batch, seq, _ = x.shape
    tm = TILE_M
    v0 = 2 * GLA_KD
    g0 = v0 + GLA_VD
    l0 = g0 + GLA_VD
    w_qkg = jnp.concatenate([w_in[:, :v0], w_in[:, g0:l0]], axis=1).astype(_BF16)
    w_vt = w_in[:, v0:g0].T.astype(_BF16)
    w_lr = jnp.pad(w_in[:, l0:], ((0, 0), (0, LANES - GLA_GATE_RANK))).astype(_BF16)
    w_gk_p = jnp.pad(w_gk, ((0, LANES - GLA_GATE_RANK), (0, 0))).astype(_BF16)
    args = (x, w_qkg, w_vt, w_lr, w_gk_p, b_gk.reshape(1, GLA_KD), norm_w.reshape(1, GLA_DV),
            w_out.astype(_BF16), ln_g.reshape(1, D_MODEL), ln_b.reshape(1, D_MODEL))
    return pl.pallas_call(
        _gla_kernel,
        grid=(batch, seq // tm),
        in_specs=[_token_spec(tm)] + [_resident()] * (len(args) - 1),
        out_specs=_token_spec(tm),
        out_shape=jax.ShapeDtypeStruct(x.shape, x.dtype),
        scratch_shapes=[
            pltpu.VMEM((tm, GLA_KD), _F32),
            pltpu.VMEM((tm, GLA_KD), _F32),
            pltpu.VMEM((tm, GLA_VD), _F32),
            pltpu.VMEM((GLA_VD, tm), _BF16),
            pltpu.VMEM((tm, GLA_KD), _F32),
            pltpu.VMEM((tm, GLA_VD), _BF16),
            pltpu.VMEM((GLA_HEADS, GLA_DV, GLA_DK), _F32),
        ],
        compiler_params=_compiler_params(),
        name="gla_layer",
    )(*args)


def _pool_layer(x, w_in, w_grp, b_grp, scale, w_out, ln_g, ln_b):
    batch, seq, _ = x.shape
    tm = TILE_M
    args = (x, w_in.astype(_BF16), w_grp.astype(_BF16), b_grp.reshape(1, POOL_WIDTH),
            scale.reshape(1, POOL_WIDTH), w_out.astype(_BF16),
            ln_g.reshape(1, D_MODEL), ln_b.reshape(1, D_MODEL))
    return pl.pallas_call(
        _pool_kernel,
        grid=(batch, seq // tm),
        in_specs=[_token_spec(tm)] + [_resident()] * (len(args) - 1),
        out_specs=_token_spec(tm),
        out_shape=jax.ShapeDtypeStruct(x.shape, x.dtype),
        scratch_shapes=[
            pltpu.VMEM((POOL_HALO, POOL_WIDTH), _F32),
            pltpu.VMEM((tm, POOL_WIDTH), _BF16),
        ],
        compiler_params=_compiler_params(),
        name="pool_layer",
    )(*args)


def kernel(x, gla_w_in, gla_w_gk, gla_b_gk, gla_norm_w, gla_w_out, pool_w_in, pool_w_grp,
           pool_b_grp, pool_scale, pool_w_out, ln_g, ln_b):
    assert x.shape[1] % TILE_M == 0 and TILE_M % GLA_BLOCK == 0
    for i in range(DEPTH):
        j = i // 2
        if i % 2 == 0:
            x = _gla_layer(x, gla_w_in[j], gla_w_gk[j], gla_b_gk[j], gla_norm_w[j],
                           gla_w_out[j], ln_g[i], ln_b[i])
        else:
            x = _pool_layer(x, pool_w_in[j], pool_w_grp[j], pool_b_grp[j], pool_scale[j],
                            pool_w_out[j], ln_g[i], ln_b[i])
    return x
```

```python
import functools

import jax
import jax.numpy as jnp
from jax import lax
from jax.experimental import pallas as pl
from jax.experimental.pallas import tpu as pltpu

D_MODEL = 1024
DEPTH = 4

GLA_HEADS = 4
GLA_KD = D_MODEL // 2
GLA_VD = D_MODEL
GLA_DK = GLA_KD // GLA_HEADS
GLA_DV = GLA_VD // GLA_HEADS
GLA_GATE_RANK = 16
GLA_GATE_NORMALIZER = 16.0
GLA_BLOCK = 128

POOL_WIDTH = 2 * D_MODEL
POOL_WINDOWS = (2, 4, 8, 16)
POOL_GROUPS = len(POOL_WINDOWS)
POOL_GW = POOL_WIDTH // POOL_GROUPS
POOL_HALO = 16

DEEPNORM_ALPHA = (2.0 * DEPTH) ** 0.25
LN_EPS = 1e-5
RMS_EPS = 1e-6

LANES = 128
MXU_N = 256
LN_ROWS = 128
TILE_M = 512
VMEM_LIMIT_BYTES = 56 * 1024 * 1024

_F32 = jnp.float32
_BF16 = jnp.bfloat16


def _dot(a, b):
    return jnp.dot(a, b, preferred_element_type=_F32)


def _dot_nt(a, b):
    return lax.dot_general(a, b, (((1,), (1,)), ((), ())), preferred_element_type=_F32)


def _silu(z):
    return z * (1.0 / (1.0 + jnp.exp(-z)))


def _residual_layernorm(x, y, g, b):
    r = DEEPNORM_ALPHA * x + y
    mu = jnp.mean(r, axis=-1, keepdims=True)
    d = r - mu
    var = jnp.mean(d * d, axis=-1, keepdims=True)
    return d * lax.rsqrt(var + LN_EPS) * g + b


def _per_parity(step, body):
    for wr in range(2):
        pl.when(step % 2 == wr)(functools.partial(body, wr, 1 - wr))


def _gla_kernel(x_ref, xlag_ref, wqkg_ref, wvt_ref, wlr_ref, wgk_ref, bgk_ref, nw_ref,
                wout_ref, lng_ref, lnb_ref, o_ref, xb_s, q_s, k_s, sg_s, vt_s, bcum_s, qk_s,
                lhs_s, y_s, out_s, st_s, *, tiles_per_seq):
    tm = x_ref.shape[0]
    n_blocks = tm // GLA_BLOCK
    n_ln = tm // LN_ROWS
    step = pl.program_id(0)

    @pl.when(step == 0)
    def _():
        q_s[1] = jnp.zeros(q_s.shape[1:], q_s.dtype)
        k_s[1] = jnp.zeros(k_s.shape[1:], k_s.dtype)
        sg_s[1] = jnp.zeros(sg_s.shape[1:], sg_s.dtype)
        vt_s[1] = jnp.zeros(vt_s.shape[1:], vt_s.dtype)
        bcum_s[1] = jnp.zeros(bcum_s.shape[1:], bcum_s.dtype)
        st_s[...] = jnp.zeros_like(st_s)

    seq_start = (step - 1) % tiles_per_seq == 0

    def body(wr, rd):
        xb_s[...] = x_ref[...].astype(_BF16)
        row = lax.broadcasted_iota(jnp.int32, (GLA_BLOCK, GLA_BLOCK), 0)
        col = lax.broadcasted_iota(jnp.int32, (GLA_BLOCK, GLA_BLOCK), 1)
        causal = row >= col

        def proj_q(j):
            c = slice(MXU_N * j, MXU_N * (j + 1))
            q_s[wr, :, c] = _dot(xb_s[...], wqkg_ref[:, c]) * (GLA_DK ** -0.5)

        def proj_k(j):
            c = slice(MXU_N * j, MXU_N * (j + 1))
            w0 = GLA_KD + MXU_N * j
            k_s[wr, :, c] = _dot(xb_s[...], wqkg_ref[:, w0:w0 + MXU_N])

        def proj_g(j):
            c = slice(MXU_N * j, MXU_N * (j + 1))
            w0 = 2 * GLA_KD + MXU_N * j
            sg_s[wr, :, c] = _silu(_dot(xb_s[...], wqkg_ref[:, w0:w0 + MXU_N]))

        def proj_v(j):
            r = slice(MXU_N * j, MXU_N * (j + 1))
            vt_s[wr, r, :] = _dot_nt(wvt_ref[r, :], xb_s[...]).astype(_BF16)

        def gate_logits():
            lr = _dot(xb_s[...], wlr_ref[...])
            bcum_s[wr] = _dot(lr.astype(_BF16), wgk_ref[...]) + bgk_ref[...]

        def gate_cumsum():
            tri = causal.astype(_BF16)
            tri2 = jnp.concatenate([tri, tri], axis=1)
            for p in range(n_blocks):
                rows = slice(p * GLA_BLOCK, (p + 1) * GLA_BLOCK)
                gk = jax.nn.log_sigmoid(bcum_s[wr, rows, :]) / GLA_GATE_NORMALIZER
                gk_hi = gk.astype(_BF16)
                gk_lo = (gk - gk_hi.astype(_F32)).astype(_BF16)
                bcum_s[wr, rows, :] = _dot(tri2, jnp.concatenate([gk_hi, gk_lo], axis=0))

        decays = {}

        def block_prep(p):
            r0 = p * GLA_BLOCK
            rows = slice(r0, r0 + GLA_BLOCK)
            b = bcum_s[rd, rows, :]
            b_mid = bcum_s[rd, r0 + GLA_BLOCK // 2 - 1:r0 + GLA_BLOCK // 2, :]
            b_last = bcum_s[rd, r0 + GLA_BLOCK - 1:r0 + GLA_BLOCK, :]
            q = q_s[rd, rows, :]
            k = k_s[rd, rows, :]
            qk_s[p,0] = (q * jnp.exp(b - b_mid)).astype(_BF16)
            qk_s[p,1] = (k * jnp.exp(b_mid - b)).astype(_BF16)
            qk_s[p,2] = (k * jnp.exp(b_last - b)).astype(_BF16)
            q_dec = (q * jnp.exp(b)).astype(_BF16)
            for h in range(GLA_HEADS):
                lhs_s[p,h, :, GLA_BLOCK:] = q_dec[:, h * GLA_DK:(h + 1) * GLA_DK]
            decays[p] = jnp.exp(b_last)

        def scores(p, h):
            hs = slice(h * GLA_DK, (h + 1) * GLA_DK)
            sc = _dot_nt(qk_s[p,0, :, hs], qk_s[p,1, :, hs])
            lhs_s[p,h, :, :GLA_BLOCK] = jnp.where(causal, sc, 0.0).astype(_BF16)

        def mix(p, h):
            rows = slice(p * GLA_BLOCK, (p + 1) * GLA_BLOCK)
            hs = slice(h * GLA_DK, (h + 1) * GLA_DK)
            vs = slice(h * GLA_DV, (h + 1) * GLA_DV)
            v_t = vt_s[rd, vs, rows]
            state_t = st_s[h]
            if p == 0:
                state_t = jnp.where(seq_start, 0.0, state_t)
            rhs = jnp.concatenate([v_t, state_t.astype(_BF16)], axis=1)
            o = _dot_nt(lhs_s[p,h], rhs)
            st_s[h] = state_t * decays[p][:, hs] + _dot(v_t, qk_s[p,2, :, hs])
            ms = jnp.mean(o * o, axis=-1, keepdims=True)
            o = o * lax.rsqrt(ms + RMS_EPS) * nw_ref[...]
            y_s[rows, vs] = (o * sg_s[rd, rows, vs]).astype(_BF16)

        def out_proj(j):
            c = slice(MXU_N * j, MXU_N * (j + 1))
            out_s[:, c] = _dot(y_s[...], wout_ref[:, c])

        def layer_norm(j):
            rows = slice(LN_ROWS * j, LN_ROWS * (j + 1))
            o_ref[rows, :] = _residual_layernorm(xlag_ref[rows, :], out_s[rows, :],
                                                 lng_ref[...], lnb_ref[...])

        P = functools.partial
        fill = ([P(proj_q, j) for j in range(GLA_KD // MXU_N)]
                + [P(proj_k, j) for j in range(GLA_KD // MXU_N)]
                + [P(proj_v, j) for j in range(GLA_VD // MXU_N)]
                + [P(proj_g, j) for j in range(GLA_VD // MXU_N)])
        heads = [(p, h) for p in range(n_blocks) for h in range(GLA_HEADS)]
        lookahead = 2

        block_prep(0)
        gate_logits()
        fill.pop(0)()
        for n in range(lookahead):
            scores(*heads[n])
        fill.pop(0)()
        gate_cumsum()
        for n, (p, h) in enumerate(heads):
            if h == 0 and p + 1 < n_blocks:
                block_prep(p + 1)
            if n + lookahead < len(heads):
                scores(*heads[n + lookahead])
            if n % 3 == 2 and len(fill) > n_ln:
                fill.pop(0)()
            mix(p, h)
        while len(fill) > n_ln:
            fill.pop(0)()
        for j in range(D_MODEL // MXU_N):
            out_proj(j)
        for j in range(n_ln):
            layer_norm(j)
            fill.pop(0)()

    _per_parity(step, body)


def _pool_kernel(x_ref, xlag_ref, win_ref, wgrp_ref, bgrp_ref, scale_ref, wout_ref,
                 lng_ref, lnb_ref, o_ref, xb_s, halo_s, ext_s, z_s, p_s, y_s, out_s,
                 *, tiles_per_seq):
    tm = x_ref.shape[0]
    step = pl.program_id(0)

    @pl.when(step == 0)
    def _():
        y_s[1] = jnp.zeros(y_s.shape[1:], y_s.dtype)
        halo_s[...] = jnp.zeros_like(halo_s)

    tile = step % tiles_per_seq
    seq_start = tile == 0

    def body(wr, rd):
        xb_s[...] = x_ref[...].astype(_BF16)
        pos = (tile * tm + lax.broadcasted_iota(jnp.int32, (tm, 1), 0) + 1).astype(_F32)

        def u_dot(g):
            u = _dot(xb_s[...], win_ref[:, g * POOL_GW:(g + 1) * POOL_GW])
            ext_s[g, :POOL_HALO, :] = jnp.where(seq_start, 0.0, halo_s[g])
            ext_s[g, POOL_HALO:, :] = u
            halo_s[g] = u[tm - POOL_HALO:, :]

        def z_dot(g):
            c0 = POOL_WIDTH + g * POOL_GW
            z_s[...] = _silu(_dot(xb_s[...], win_ref[:, c0:c0 + POOL_GW]))

        def pool(g):
            inv_cnt = 1.0 / jnp.minimum(pos, float(POOL_WINDOWS[g]))
            s = ext_s[g]
            for lvl in range(g + 1):
                s = s + pltpu.roll(s, 2 ** lvl, axis=0)
            p_s[...] = (s[POOL_HALO:, :] * inv_cnt - ext_s[g, POOL_HALO:, :]).astype(_BF16)

        def group_gate(g):
            cs = slice(g * POOL_GW, (g + 1) * POOL_GW)
            m = _dot(p_s[...], wgrp_ref[g]) + bgrp_ref[:, cs]
            m = m * scale_ref[:, cs]
            y_s[wr, :, cs] = (m * z_s[...]).astype(_BF16)

        def out_proj(j):
            c = slice(MXU_N * j, MXU_N * (j + 1))
            out_s[:, c] = _dot(y_s[rd], wout_ref[:, c])

        def layer_norm(j):
            rows = slice(LN_ROWS * j, LN_ROWS * (j + 1))
            o_ref[rows, :] = _residual_layernorm(xlag_ref[rows, :], out_s[rows, :],
                                                 lng_ref[...], lnb_ref[...])

        for j in range(D_MODEL // MXU_N):
            out_proj(j)
        assert tm // LN_ROWS == POOL_GROUPS
        for g in range(POOL_GROUPS):
            u_dot(g)
            layer_norm(g)
            z_dot(g)
            pool(g)
            group_gate(g)

    _per_parity(step, body)


def _resident():
    return pl.BlockSpec(memory_space=pltpu.VMEM)


def _token_specs(tm, tiles_per_seq, n_tiles):
    def cur(s):
        t = jnp.minimum(s, n_tiles - 1)
        return (t // tiles_per_seq, t % tiles_per_seq, 0)

    def lag(s):
        t = jnp.maximum(s - 1, 0)
        return (t // tiles_per_seq, t % tiles_per_seq, 0)

    block = (None, tm, D_MODEL)
    return pl.BlockSpec(block, cur), pl.BlockSpec(block, lag)


def _compiler_params():
    return pltpu.CompilerParams(
        dimension_semantics=("arbitrary",),
        vmem_limit_bytes=VMEM_LIMIT_BYTES,
    )


def _gla_layer(x, w_in, w_gk, b_gk, norm_w, w_out, ln_g, ln_b):
    batch, seq, _ = x.shape
    tm = TILE_M
    tiles_per_seq = seq // tm
    n_tiles = batch * tiles_per_seq
    v0 = 2 * GLA_KD
    g0 = v0 + GLA_VD
    l0 = g0 + GLA_VD
    w_qkg = jnp.concatenate([w_in[:, :v0], w_in[:, g0:l0]], axis=1).astype(_BF16)
    w_vt = w_in[:, v0:g0].T.astype(_BF16)
    w_lr = jnp.pad(w_in[:, l0:], ((0, 0), (0, LANES - GLA_GATE_RANK))).astype(_BF16)
    w_gk_p = jnp.pad(w_gk, ((0, LANES - GLA_GATE_RANK), (0, 0))).astype(_BF16)
    weights = (w_qkg, w_vt, w_lr, w_gk_p, b_gk.reshape(1, GLA_KD), norm_w.reshape(1, GLA_DV),
               w_out.astype(_BF16), ln_g.reshape(1, D_MODEL), ln_b.reshape(1, D_MODEL))
    cur, lag = _token_specs(tm, tiles_per_seq, n_tiles)
    return pl.pallas_call(
        functools.partial(_gla_kernel, tiles_per_seq=tiles_per_seq),
        grid=(n_tiles + 1,),
        in_specs=[cur, lag] + [_resident()] * len(weights),
        out_specs=lag,
        out_shape=jax.ShapeDtypeStruct(x.shape, x.dtype),
        scratch_shapes=[
            pltpu.VMEM((tm, D_MODEL), _BF16),
            pltpu.VMEM((2, tm, GLA_KD), _F32),
            pltpu.VMEM((2, tm, GLA_KD), _F32),
            pltpu.VMEM((2, tm, GLA_VD), _F32),
            pltpu.VMEM((2, GLA_VD, tm), _BF16),
            pltpu.VMEM((2, tm, GLA_KD), _F32),
            pltpu.VMEM((tm // GLA_BLOCK, 3, GLA_BLOCK, GLA_KD), _BF16),
            pltpu.VMEM((tm // GLA_BLOCK, GLA_HEADS, GLA_BLOCK, 2 * GLA_BLOCK), _BF16),
            pltpu.VMEM((tm, GLA_VD), _BF16),
            pltpu.VMEM((tm, D_MODEL), _F32),
            pltpu.VMEM((GLA_HEADS, GLA_DV, GLA_DK), _F32),
        ],
        compiler_params=_compiler_params(),
        name="gla_layer",
    )(x, x, *weights)


def _pool_layer(x, w_in, w_grp, b_grp, scale, w_out, ln_g, ln_b):
    batch, seq, _ = x.shape
    tm = TILE_M
    tiles_per_seq = seq // tm
    n_tiles = batch * tiles_per_seq
    weights = (w_in.astype(_BF16), w_grp.astype(_BF16), b_grp.reshape(1, POOL_WIDTH),
               scale.reshape(1, POOL_WIDTH), w_out.astype(_BF16),
               ln_g.reshape(1, D_MODEL), ln_b.reshape(1, D_MODEL))
    cur, lag = _token_specs(tm, tiles_per_seq, n_tiles)
    return pl.pallas_call(
        functools.partial(_pool_kernel, tiles_per_seq=tiles_per_seq),
        grid=(n_tiles + 1,),
        in_specs=[cur, lag] + [_resident()] * len(weights),
        out_specs=lag,
        out_shape=jax.ShapeDtypeStruct(x.shape, x.dtype),
        scratch_shapes=[
            pltpu.VMEM((tm, D_MODEL), _BF16),
            pltpu.VMEM((POOL_GROUPS, POOL_HALO, POOL_GW), _F32),
            pltpu.VMEM((POOL_GROUPS, POOL_HALO + tm, POOL_GW), _F32),
            pltpu.VMEM((tm, POOL_GW), _F32),
            pltpu.VMEM((tm, POOL_GW), _BF16),
            pltpu.VMEM((2, tm, POOL_WIDTH), _BF16),
            pltpu.VMEM((tm, D_MODEL), _F32),
        ],
        compiler_params=_compiler_params(),
        name="pool_layer",
    )(x, x, *weights)


def kernel(x, gla_w_in, gla_w_gk, gla_b_gk, gla_norm_w, gla_w_out, pool_w_in, pool_w_grp,
           pool_b_grp, pool_scale, pool_w_out, ln_g, ln_b):
    assert x.shape[1] % TILE_M == 0 and TILE_M % GLA_BLOCK == 0
    for i in range(DEPTH):
        j = i // 2
        if i % 2 == 0:
            x = _gla_layer(x, gla_w_in[j], gla_w_gk[j], gla_b_gk[j], gla_norm_w[j],
                           gla_w_out[j], ln_g[i], ln_b[i])
        else:
            x = _pool_layer(x, pool_w_in[j], pool_w_grp[j], pool_b_grp[j], pool_scale[j],
                            pool_w_out[j], ln_g[i], ln_b[i])
    return x
```

```python
import functools

import jax
import jax.numpy as jnp
from jax import lax
from jax.experimental import pallas as pl
from jax.experimental.pallas import tpu as pltpu

D_MODEL = 1024
DEPTH = 4

GLA_HEADS = 4
GLA_KD = D_MODEL // 2
GLA_VD = D_MODEL
GLA_DK = GLA_KD // GLA_HEADS
GLA_DV = GLA_VD // GLA_HEADS
GLA_GATE_RANK = 16
GLA_GATE_NORMALIZER = 16.0
GLA_IN = 2 * GLA_KD + 2 * GLA_VD + GLA_GATE_RANK
GLA_K0 = GLA_KD
GLA_V0 = 2 * GLA_KD
GLA_G0 = GLA_V0 + GLA_VD
GLA_L0 = GLA_G0 + GLA_VD
GLA_BLOCK = 128

POOL_WIDTH = 2 * D_MODEL
POOL_WINDOWS = (2, 4, 8, 16)
POOL_GROUPS = len(POOL_WINDOWS)
POOL_GW = POOL_WIDTH // POOL_GROUPS
POOL_HALO = 16

DEEPNORM_ALPHA = (2.0 * DEPTH) ** 0.25
LN_EPS = 1e-5
RMS_EPS = 1e-6

LANES = 128
MXU_N = 256
LN_ROWS = 128
TILE_M = 512
VMEM_LIMIT_BYTES = 56 * 1024 * 1024

GLA_IN_PAD = GLA_L0 + LANES

_F32 = jnp.float32
_BF16 = jnp.bfloat16


def _dot(a, b):
    return jnp.dot(a, b, preferred_element_type=_F32)


def _dot_nt(a, b):
    return lax.dot_general(a, b, (((1,), (1,)), ((), ())), preferred_element_type=_F32)


def _silu(z):
    return z * (1.0 / (1.0 + jnp.exp(-z)))


def _residual_layernorm(x, y, g, b):
    r = DEEPNORM_ALPHA * x + y
    mu = jnp.mean(r, axis=-1, keepdims=True)
    d = r - mu
    var = jnp.mean(d * d, axis=-1, keepdims=True)
    return d * lax.rsqrt(var + LN_EPS) * g + b


def _per_parity(step, body):
    for wr in range(2):
        pl.when(step % 2 == wr)(functools.partial(body, wr, 1 - wr))


def _gla_kernel(x_ref, xlag_ref, win_ref, wgk_ref, bgk_ref, nw_ref, wout_ref, lng_ref, lnb_ref,
                o_ref, wvt_s, xb_s, q_s, k_s, sg_s, vt_s, bcum_s, qk_s, lhs_s, y_s, out_s, st_s,
                *, tiles_per_seq):
    tm = x_ref.shape[0]
    n_blocks = tm // GLA_BLOCK
    n_ln = tm // LN_ROWS
    step = pl.program_id(0)

    @pl.when(step == 0)
    def _():
        for r in range(GLA_VD // MXU_N):
            for c in range(D_MODEL // MXU_N):
                blk = win_ref[c * MXU_N:(c + 1) * MXU_N, GLA_V0 + r * MXU_N:GLA_V0 + (r + 1) * MXU_N]
                wvt_s[r * MXU_N:(r + 1) * MXU_N, c * MXU_N:(c + 1) * MXU_N] = (
                    blk.astype(_F32).T.astype(_BF16))
        q_s[1] = jnp.zeros(q_s.shape[1:], q_s.dtype)
        k_s[1] = jnp.zeros(k_s.shape[1:], k_s.dtype)
        sg_s[1] = jnp.zeros(sg_s.shape[1:], sg_s.dtype)
        vt_s[1] = jnp.zeros(vt_s.shape[1:], vt_s.dtype)
        bcum_s[1] = jnp.zeros(bcum_s.shape[1:], bcum_s.dtype)
        st_s[...] = jnp.zeros_like(st_s)

    seq_start = (step - 1) % tiles_per_seq == 0

    def body(wr, rd):
        xb_s[...] = x_ref[...].astype(_BF16)
        row = lax.broadcasted_iota(jnp.int32, (GLA_BLOCK, GLA_BLOCK), 0)
        col = lax.broadcasted_iota(jnp.int32, (GLA_BLOCK, GLA_BLOCK), 1)
        causal = row >= col

        def proj_q(j):
            c = slice(MXU_N * j, MXU_N * (j + 1))
            q_s[wr, :, c] = _dot(xb_s[...], win_ref[:, c]) * (GLA_DK ** -0.5)

        def proj_k(j):
            c = slice(MXU_N * j, MXU_N * (j + 1))
            w0 = GLA_K0 + MXU_N * j
            k_s[wr, :, c] = _dot(xb_s[...], win_ref[:, w0:w0 + MXU_N])

        def proj_g(j):
            c = slice(MXU_N * j, MXU_N * (j + 1))
            w0 = GLA_G0 + MXU_N * j
            sg_s[wr, :, c] = _silu(_dot(xb_s[...], win_ref[:, w0:w0 + MXU_N]))

        def proj_v(j):
            r = slice(MXU_N * j, MXU_N * (j + 1))
            vt_s[wr, r, :] = _dot_nt(wvt_s[r, :], xb_s[...]).astype(_BF16)

        def gate_logits():
            lr = _dot(xb_s[...], win_ref[:, GLA_L0:GLA_L0 + LANES])
            bcum_s[wr] = _dot(lr.astype(_BF16), wgk_ref[...]) + bgk_ref[...]

        def gate_cumsum():
            tri = causal.astype(_BF16)
            tri2 = jnp.concatenate([tri, tri], axis=1)
            for p in range(n_blocks):
                rows = slice(p * GLA_BLOCK, (p + 1) * GLA_BLOCK)
                gk = jax.nn.log_sigmoid(bcum_s[wr, rows, :]) / GLA_GATE_NORMALIZER
                gk_hi = gk.astype(_BF16)
                gk_lo = (gk - gk_hi.astype(_F32)).astype(_BF16)
                bcum_s[wr, rows, :] = _dot(tri2, jnp.concatenate([gk_hi, gk_lo], axis=0))

        decays = {}

        def block_prep(p):
            r0 = p * GLA_BLOCK
            rows = slice(r0, r0 + GLA_BLOCK)
            b = bcum_s[rd, rows, :]
            b_mid = bcum_s[rd, r0 + GLA_BLOCK // 2 - 1:r0 + GLA_BLOCK // 2, :]
            b_last = bcum_s[rd, r0 + GLA_BLOCK - 1:r0 + GLA_BLOCK, :]
            q = q_s[rd, rows, :]
            k = k_s[rd, rows, :]
            qk_s[p, 0] = (q * jnp.exp(b - b_mid)).astype(_BF16)
            qk_s[p, 1] = (k * jnp.exp(b_mid - b)).astype(_BF16)
            qk_s[p, 2] = (k * jnp.exp(b_last - b)).astype(_BF16)
            q_dec = (q * jnp.exp(b)).astype(_BF16)
            for h in range(GLA_HEADS):
                lhs_s[p, h, :, GLA_BLOCK:] = q_dec[:, h * GLA_DK:(h + 1) * GLA_DK]
            decays[p] = jnp.exp(b_last)

        def scores(p, h):
            hs = slice(h * GLA_DK, (h + 1) * GLA_DK)
            sc = _dot_nt(qk_s[p, 0, :, hs], qk_s[p, 1, :, hs])
            lhs_s[p, h, :, :GLA_BLOCK] = jnp.where(causal, sc, 0.0).astype(_BF16)

        def mix(p, h):
            rows = slice(p * GLA_BLOCK, (p + 1) * GLA_BLOCK)
            hs = slice(h * GLA_DK, (h + 1) * GLA_DK)
            vs = slice(h * GLA_DV, (h + 1) * GLA_DV)
            v_t = vt_s[rd, vs, rows]
            state_t = st_s[h]
            if p == 0:
                state_t = jnp.where(seq_start, 0.0, state_t)
            rhs = jnp.concatenate([v_t, state_t.astype(_BF16)], axis=1)
            o = _dot_nt(lhs_s[p, h], rhs)
            st_s[h] = state_t * decays[p][:, hs] + _dot(v_t, qk_s[p, 2, :, hs])
            ms = jnp.mean(o * o, axis=-1, keepdims=True)
            o = o * lax.rsqrt(ms + RMS_EPS) * nw_ref[...]
            y_s[rows, vs] = (o * sg_s[rd, rows, vs]).astype(_BF16)

        def out_proj(j):
            c = slice(MXU_N * j, MXU_N * (j + 1))
            out_s[:, c] = _dot(y_s[...], wout_ref[:, c])

        def layer_norm(j):
            rows = slice(LN_ROWS * j, LN_ROWS * (j + 1))
            o_ref[rows, :] = _residual_layernorm(xlag_ref[rows, :], out_s[rows, :],
                                                 lng_ref[...], lnb_ref[...])

        P = functools.partial
        fill = ([P(proj_q, j) for j in range(GLA_KD // MXU_N)]
                + [P(proj_k, j) for j in range(GLA_KD // MXU_N)]
                + [P(proj_v, j) for j in range(GLA_VD // MXU_N)]
                + [P(proj_g, j) for j in range(GLA_VD // MXU_N)])
        heads = [(p, h) for p in range(n_blocks) for h in range(GLA_HEADS)]
        lookahead = 2

        gate_logits()
        fill.pop(0)()
        block_prep(0)
        for n in range(lookahead):
            scores(*heads[n])
        fill.pop(0)()
        gate_cumsum()
        for n, (p, h) in enumerate(heads):
            if h == 0 and p + 1 < n_blocks:
                block_prep(p + 1)
            if n + lookahead < len(heads):
                scores(*heads[n + lookahead])
            if n % 3 == 2 and len(fill) > n_ln:
                fill.pop(0)()
            mix(p, h)
        while len(fill) > n_ln:
            fill.pop(0)()
        for j in range(D_MODEL // MXU_N):
            out_proj(j)
        for j in range(n_ln):
            layer_norm(j)
            fill.pop(0)()

    _per_parity(step, body)


def _pool_kernel(x_ref, win_ref, wgrp_ref, bgrp_ref, scale_ref, wout_ref, lng_ref, lnb_ref,
                 o_ref, halo_s, y_s):
    tm = x_ref.shape[0]
    i = pl.program_id(1)

    @pl.when(i == 0)
    def _():
        halo_s[...] = jnp.zeros_like(halo_s)

    x = x_ref[...]
    xb = x.astype(_BF16)
    pos = (i * tm + lax.broadcasted_iota(jnp.int32, (tm, 1), 0) + 1).astype(_F32)

    for g, w in enumerate(POOL_WINDOWS):
        cs = slice(g * POOL_GW, (g + 1) * POOL_GW)
        u = _dot(xb, win_ref[:, cs])
        ext = jnp.concatenate([halo_s[:, cs], u], axis=0)
        halo_s[:, cs] = u[tm - POOL_HALO:, :]
        s = ext
        for lvl in range(g + 1):
            s = s + pltpu.roll(s, 2 ** lvl, axis=0)
        inv_cnt = 1.0 / jnp.minimum(pos, float(w))
        pooled = s[POOL_HALO:, :] * inv_cnt - u
        m = _dot(pooled.astype(_BF16), wgrp_ref[g]) + bgrp_ref[:, cs]
        m = m * scale_ref[:, cs]
        z = _dot(xb, win_ref[:, POOL_WIDTH + g * POOL_GW:POOL_WIDTH + (g + 1) * POOL_GW])
        y_s[:, cs] = (m * _silu(z)).astype(_BF16)

    out = _dot(y_s[...], wout_ref[...])
    o_ref[...] = _residual_layernorm(x, out, lng_ref[...], lnb_ref[...])


def _layer_spec(stacked, layer, grid_rank):
    zeros = (0,) * (stacked.ndim - 1)
    return pl.BlockSpec((None,) + stacked.shape[1:], lambda *_: (layer,) + zeros,
                        pipeline_mode=pl.Buffered(1))


def _gla_layer(x, layer, ln_layer, w_in, w_gk, b_gk, norm_w, w_out, ln_g, ln_b):
    batch, seq, _ = x.shape
    tm = TILE_M
    tiles_per_seq = seq // tm
    n_tiles = batch * tiles_per_seq

    def cur(s):
        t = jnp.minimum(s, n_tiles - 1)
        return (t // tiles_per_seq, t % tiles_per_seq, 0)

    def lag(s):
        t = jnp.maximum(s - 1, 0)
        return (t // tiles_per_seq, t % tiles_per_seq, 0)

    block = (None, tm, D_MODEL)
    params = [(w_in, layer), (w_gk, layer), (b_gk, layer), (norm_w, layer), (w_out, layer),
              (ln_g, ln_layer), (ln_b, ln_layer)]
    return pl.pallas_call(
        functools.partial(_gla_kernel, tiles_per_seq=tiles_per_seq),
        grid=(n_tiles + 1,),
        in_specs=[pl.BlockSpec(block, cur), pl.BlockSpec(block, lag)]
                 + [_layer_spec(a, l, 1) for a, l in params],
        out_specs=pl.BlockSpec(block, lag),
        out_shape=jax.ShapeDtypeStruct(x.shape, x.dtype),
        scratch_shapes=[
            pltpu.VMEM((GLA_VD, D_MODEL), _BF16),
            pltpu.VMEM((tm, D_MODEL), _BF16),
            pltpu.VMEM((2, tm, GLA_KD), _F32),
            pltpu.VMEM((2, tm, GLA_KD), _F32),
            pltpu.VMEM((2, tm, GLA_VD), _F32),
            pltpu.VMEM((2, GLA_VD, tm), _BF16),
            pltpu.VMEM((2, tm, GLA_KD), _F32),
            pltpu.VMEM((tm // GLA_BLOCK, 3, GLA_BLOCK, GLA_KD), _BF16),
            pltpu.VMEM((tm // GLA_BLOCK, GLA_HEADS, GLA_BLOCK, 2 * GLA_BLOCK), _BF16),
            pltpu.VMEM((tm, GLA_VD), _BF16),
            pltpu.VMEM((tm, D_MODEL), _F32),
            pltpu.VMEM((GLA_HEADS, GLA_DV, GLA_DK), _F32),
        ],
        compiler_params=pltpu.CompilerParams(dimension_semantics=("arbitrary",),
                                             vmem_limit_bytes=VMEM_LIMIT_BYTES),
        name="gla_layer",
    )(x, x, *[a for a, _ in params])


def _pool_layer(x, layer, ln_layer, w_in, w_grp, b_grp, scale, w_out, ln_g, ln_b):
    batch, seq, _ = x.shape
    tm = TILE_M
    token_spec = pl.BlockSpec((None, tm, D_MODEL), lambda b, i: (b, i, 0))
    params = [(w_in, layer), (w_grp, layer), (b_grp, layer), (scale, layer), (w_out, layer),
              (ln_g, ln_layer), (ln_b, ln_layer)]
    return pl.pallas_call(
        _pool_kernel,
        grid=(batch, seq // tm),
        in_specs=[token_spec] + [_layer_spec(a, l, 2) for a, l in params],
        out_specs=token_spec,
        out_shape=jax.ShapeDtypeStruct(x.shape, x.dtype),
        scratch_shapes=[
            pltpu.VMEM((POOL_HALO, POOL_WIDTH), _F32),
            pltpu.VMEM((tm, POOL_WIDTH), _BF16),
        ],
        compiler_params=pltpu.CompilerParams(dimension_semantics=("arbitrary", "arbitrary"),
                                             vmem_limit_bytes=VMEM_LIMIT_BYTES),
        name="pool_layer",
    )(x, *[a for a, _ in params])


def kernel(x, gla_w_in, gla_w_gk, gla_b_gk, gla_norm_w, gla_w_out, pool_w_in, pool_w_grp,
           pool_b_grp, pool_scale, pool_w_out, ln_g, ln_b):
    assert x.shape[1] % TILE_M == 0 and TILE_M % GLA_BLOCK == 0
    gla = (jnp.pad(gla_w_in, ((0, 0), (0, 0), (0, GLA_IN_PAD - GLA_IN))).astype(_BF16),
           jnp.pad(gla_w_gk, ((0, 0), (0, LANES - GLA_GATE_RANK), (0, 0))).astype(_BF16),
           gla_b_gk[:, None, :], gla_norm_w[:, None, :], gla_w_out.astype(_BF16))
    pool = (pool_w_in.astype(_BF16), pool_w_grp.astype(_BF16), pool_b_grp.reshape(-1, 1, POOL_WIDTH),
            pool_scale[:, None, :], pool_w_out.astype(_BF16))
    ln = (ln_g[:, None, :], ln_b[:, None, :])
    for i in range(DEPTH):
        layer = i // 2
        x = (_gla_layer if i % 2 == 0 else _pool_layer)(x, layer, i, *(gla if i % 2 == 0 else pool), *ln)
    return x
```

```python
import functools

import jax
import jax.numpy as jnp
from jax import lax
from jax.experimental import pallas as pl
from jax.experimental.pallas import tpu as pltpu

D_MODEL = 1024
DEPTH = 4

GLA_HEADS = 4
GLA_KD = D_MODEL // 2
GLA_VD = D_MODEL
GLA_DK = GLA_KD // GLA_HEADS
GLA_DV = GLA_VD // GLA_HEADS
GLA_GATE_RANK = 16
GLA_GATE_NORMALIZER = 16.0
GLA_IN = 2 * GLA_KD + 2 * GLA_VD + GLA_GATE_RANK
GLA_K0 = GLA_KD
GLA_V0 = 2 * GLA_KD
GLA_G0 = GLA_V0 + GLA_VD
GLA_L0 = GLA_G0 + GLA_VD
GLA_BLOCK = 128

POOL_WIDTH = 2 * D_MODEL
POOL_WINDOWS = (2, 4, 8, 16)
POOL_GROUPS = len(POOL_WINDOWS)
POOL_GW = POOL_WIDTH // POOL_GROUPS
POOL_HALO = 16

DEEPNORM_ALPHA = (2.0 * DEPTH) ** 0.25
LN_EPS = 1e-5
RMS_EPS = 1e-6

LANES = 128
MXU_N = 256
PROJ_N = 2 * MXU_N
LN_ROWS = 128
TILE_M = 512
VMEM_LIMIT_BYTES = 56 * 1024 * 1024

_F32 = jnp.float32
_BF16 = jnp.bfloat16


def _dot(a, b):
    return jnp.dot(a, b, preferred_element_type=_F32)


def _dot_nt(a, b):
    return lax.dot_general(a, b, (((1,), (1,)), ((), ())), preferred_element_type=_F32)


def _silu(z):
    return z * (1.0 / (1.0 + jnp.exp(-z)))


def _residual_layernorm(x, y, g, b):
    r = DEEPNORM_ALPHA * x + y
    mu = jnp.mean(r, axis=-1, keepdims=True)
    d = r - mu
    var = jnp.mean(d * d, axis=-1, keepdims=True)
    return d * lax.rsqrt(var + LN_EPS) * g + b


def _per_parity(step, body):
    for wr in range(2):
        pl.when(step % 2 == wr)(functools.partial(body, wr, 1 - wr))


def _gla_kernel(x_ref, xlag_ref, win_ref, wlr_ref, wgk_ref, bgk_ref, nw_ref, wout_ref, lng_ref,
                lnb_ref, o_ref, wvt_s, xb_s, lr_s, q_s, k_s, sg_s, vt_s, bcum_s, qk_s, lhs_s, y_s,
                out_s, st_s, *, tiles_per_seq):
    tm = x_ref.shape[0]
    n_blocks = tm // GLA_BLOCK
    n_ln = tm // LN_ROWS
    step = pl.program_id(0)

    @pl.when(step == 0)
    def _():
        for r in range(GLA_VD // MXU_N):
            for c in range(D_MODEL // MXU_N):
                blk = win_ref[c * MXU_N:(c + 1) * MXU_N, GLA_V0 + r * MXU_N:GLA_V0 + (r + 1) * MXU_N]
                wvt_s[r * MXU_N:(r + 1) * MXU_N, c * MXU_N:(c + 1) * MXU_N] = (
                    blk.astype(_F32).T.astype(_BF16))
        q_s[1] = jnp.zeros(q_s.shape[1:], q_s.dtype)
        k_s[1] = jnp.zeros(k_s.shape[1:], k_s.dtype)
        sg_s[1] = jnp.zeros(sg_s.shape[1:], sg_s.dtype)
        vt_s[1] = jnp.zeros(vt_s.shape[1:], vt_s.dtype)
        bcum_s[1] = jnp.zeros(bcum_s.shape[1:], bcum_s.dtype)
        st_s[...] = jnp.zeros_like(st_s)

    seq_start = (step - 1) % tiles_per_seq == 0

    def body(wr, rd):
        xb_s[...] = x_ref[...].astype(_BF16)
        row = lax.broadcasted_iota(jnp.int32, (GLA_BLOCK, GLA_BLOCK), 0)
        col = lax.broadcasted_iota(jnp.int32, (GLA_BLOCK, GLA_BLOCK), 1)
        causal = row >= col

        def proj_q(j):
            c = slice(PROJ_N * j, PROJ_N * (j + 1))
            q_s[wr, :, c] = _dot(xb_s[...], win_ref[:, c]) * (GLA_DK ** -0.5)

        def proj_k(j):
            c = slice(PROJ_N * j, PROJ_N * (j + 1))
            w0 = GLA_K0 + PROJ_N * j
            k_s[wr, :, c] = _dot(xb_s[...], win_ref[:, w0:w0 + PROJ_N])

        def proj_g(j):
            c = slice(PROJ_N * j, PROJ_N * (j + 1))
            w0 = GLA_G0 + PROJ_N * j
            sg_s[wr, :, c] = _silu(_dot(xb_s[...], win_ref[:, w0:w0 + PROJ_N]))

        def proj_v(j):
            t = slice(MXU_N * j, MXU_N * (j + 1))
            vt_s[wr, :, t] = _dot_nt(wvt_s[...], xb_s[t, :]).astype(_BF16)

        def gate_low_rank():
            lr_s[...] = _dot(xb_s[...], wlr_ref[...]).astype(_BF16)

        def gate_logits():
            bcum_s[wr] = _dot(lr_s[...], wgk_ref[...]) + bgk_ref[...]

        def gate_cumsum():
            tri = causal.astype(_BF16)
            tri2 = jnp.concatenate([tri, tri], axis=1)
            for p in range(n_blocks):
                rows = slice(p * GLA_BLOCK, (p + 1) * GLA_BLOCK)
                gk = jax.nn.log_sigmoid(bcum_s[wr, rows, :]) / GLA_GATE_NORMALIZER
                gk_hi = gk.astype(_BF16)
                gk_lo = (gk - gk_hi.astype(_F32)).astype(_BF16)
                bcum_s[wr, rows, :] = _dot(tri2, jnp.concatenate([gk_hi, gk_lo], axis=0))

        decays = {}

        def block_prep(p):
            r0 = p * GLA_BLOCK
            rows = slice(r0, r0 + GLA_BLOCK)
            b = bcum_s[rd, rows, :]
            b_mid = bcum_s[rd, r0 + GLA_BLOCK // 2 - 1:r0 + GLA_BLOCK // 2, :]
            b_last = bcum_s[rd, r0 + GLA_BLOCK - 1:r0 + GLA_BLOCK, :]
            q = q_s[rd, rows, :]
            k = k_s[rd, rows, :]
            qk_s[p, 0] = (q * jnp.exp(b - b_mid)).astype(_BF16)
            qk_s[p, 1] = (k * jnp.exp(b_mid - b)).astype(_BF16)
            qk_s[p, 2] = (k * jnp.exp(b_last - b)).astype(_BF16)
            q_dec = (q * jnp.exp(b)).astype(_BF16)
            for h in range(GLA_HEADS):
                lhs_s[p, h, :, GLA_BLOCK:] = q_dec[:, h * GLA_DK:(h + 1) * GLA_DK]
            decays[p] = jnp.exp(b_last)

        def scores(p, h):
            hs = slice(h * GLA_DK, (h + 1) * GLA_DK)
            sc = _dot_nt(qk_s[p, 0, :, hs], qk_s[p, 1, :, hs])
            lhs_s[p, h, :, :GLA_BLOCK] = jnp.where(causal, sc, 0.0).astype(_BF16)

        def mix(p, h):
            rows = slice(p * GLA_BLOCK, (p + 1) * GLA_BLOCK)
            hs = slice(h * GLA_DK, (h + 1) * GLA_DK)
            vs = slice(h * GLA_DV, (h + 1) * GLA_DV)
            v_t = vt_s[rd, vs, rows]
            state_t = st_s[h]
            if p == 0:
                state_t = jnp.where(seq_start, 0.0, state_t)
            rhs = jnp.concatenate([v_t, state_t.astype(_BF16)], axis=1)
            o = _dot_nt(lhs_s[p, h], rhs)
            st_s[h] = state_t * decays[p][:, hs] + _dot(v_t, qk_s[p, 2, :, hs])
            ms = jnp.mean(o * o, axis=-1, keepdims=True)
            o = o * lax.rsqrt(ms + RMS_EPS) * nw_ref[...]
            y_s[rows, vs] = (o * sg_s[rd, rows, vs]).astype(_BF16)

        def out_proj(j):
            c = slice(PROJ_N * j, PROJ_N * (j + 1))
            out_s[:, c] = _dot(y_s[...], wout_ref[:, c])

        def layer_norm(j):
            rows = slice(LN_ROWS * j, LN_ROWS * (j + 1))
            o_ref[rows, :] = _residual_layernorm(xlag_ref[rows, :], out_s[rows, :],
                                                 lng_ref[...], lnb_ref[...])

        heads = [(p, h) for p in range(n_blocks) for h in range(GLA_HEADS)]
        gate_fill = [functools.partial(proj_g, j) for j in range(GLA_VD // PROJ_N)]
        fill_every = len(heads) // len(gate_fill)
        lookahead = 2

        prep_ahead = 2
        gate_low_rank()
        for j in range(GLA_KD // PROJ_N):
            proj_q(j)
        for p in range(min(prep_ahead, n_blocks)):
            block_prep(p)
        gate_logits()
        for n in range(lookahead):
            scores(*heads[n])
        for j in range(GLA_KD // PROJ_N):
            proj_k(j)
        gate_cumsum()
        for n, (p, h) in enumerate(heads):
            if h == 0 and p + prep_ahead < n_blocks:
                block_prep(p + prep_ahead)
            if n + lookahead < len(heads):
                scores(*heads[n + lookahead])
            if n % fill_every == fill_every // 2:
                gate_fill.pop(0)()
            mix(p, h)
        assert not gate_fill
        for j in range(D_MODEL // PROJ_N):
            out_proj(j)
        layer_norm(0)
        for j in range(tm // MXU_N):
            proj_v(j)
        for j in range(1, n_ln):
            layer_norm(j)

    _per_parity(step, body)


def _pool_kernel(x_ref, xlag_ref, win_ref, wgrp_ref, bgrp_ref, scale_ref, wout_ref, lng_ref,
                 lnb_ref, o_ref, halo_s, y_s, out_s, *, tiles_per_seq):
    tm = x_ref.shape[0]
    step = pl.program_id(0)
    i = step % tiles_per_seq

    @pl.when(step == 0)
    def _():
        out_s[...] = jnp.zeros_like(out_s)
        halo_s[...] = jnp.zeros_like(halo_s)

    xb = x_ref[...].astype(_BF16)
    pos = (i * tm + lax.broadcasted_iota(jnp.int32, (tm, 1), 0) + 1).astype(_F32)
    assert tm // LN_ROWS == POOL_GROUPS

    def project(c0):
        return _dot(xb, win_ref[:, c0:c0 + POOL_GW])

    u_next = project(0)
    for g, w in enumerate(POOL_WINDOWS):
        cs = slice(g * POOL_GW, (g + 1) * POOL_GW)
        u = u_next
        rows = slice(LN_ROWS * g, LN_ROWS * (g + 1))
        o_ref[rows, :] = _residual_layernorm(xlag_ref[rows, :], out_s[rows, :],
                                             lng_ref[...], lnb_ref[...])
        z = project(POOL_WIDTH + g * POOL_GW)
        if g + 1 < POOL_GROUPS:
            u_next = project((g + 1) * POOL_GW)
        halo = jnp.where(i == 0, 0.0, halo_s[:, cs])
        ext = jnp.concatenate([halo, u], axis=0)
        halo_s[:, cs] = u[tm - POOL_HALO:, :]
        s = ext
        for lvl in range(g + 1):
            s = s + pltpu.roll(s, 2 ** lvl, axis=0)
        inv_cnt = 1.0 / jnp.minimum(pos, float(w))
        pooled = s[POOL_HALO:, :] * inv_cnt - u
        m = _dot(pooled.astype(_BF16), wgrp_ref[g]) + bgrp_ref[:, cs]
        m = m * scale_ref[:, cs]
        y_s[:, cs] = (m * _silu(z)).astype(_BF16)

    out_s[...] = _dot(y_s[...], wout_ref[...])


def _layer_spec(stacked, layer):
    zeros = (0,) * (stacked.ndim - 1)
    return pl.BlockSpec((None,) + stacked.shape[1:], lambda *_: (layer,) + zeros,
                        pipeline_mode=pl.Buffered(1))


def _skewed_layer_call(body, name, x, params, scratch_shapes):
    batch, seq, _ = x.shape
    tm = TILE_M
    tiles_per_seq = seq // tm
    n_tiles = batch * tiles_per_seq

    def cur(s):
        t = jnp.minimum(s, n_tiles - 1)
        return (t // tiles_per_seq, t % tiles_per_seq, 0)

    def lag(s):
        t = jnp.maximum(s - 1, 0)
        return (t // tiles_per_seq, t % tiles_per_seq, 0)

    block = (None, tm, D_MODEL)
    return pl.pallas_call(
        functools.partial(body, tiles_per_seq=tiles_per_seq),
        grid=(n_tiles + 1,),
        in_specs=[pl.BlockSpec(block, cur), pl.BlockSpec(block, lag)]
                 + [_layer_spec(a, l) for a, l in params],
        out_specs=pl.BlockSpec(block, lag),
        out_shape=jax.ShapeDtypeStruct(x.shape, x.dtype),
        scratch_shapes=scratch_shapes,
        compiler_params=pltpu.CompilerParams(dimension_semantics=("arbitrary",),
                                             vmem_limit_bytes=VMEM_LIMIT_BYTES),
        name=name,
    )(x, x, *[a for a, _ in params])


def _gla_layer(x, layer, ln_layer, w_in, w_lr, w_gk, b_gk, norm_w, w_out, ln_g, ln_b):
    tm = TILE_M
    params = [(w_in, layer), (w_lr, layer), (w_gk, layer), (b_gk, layer), (norm_w, layer),
              (w_out, layer), (ln_g, ln_layer), (ln_b, ln_layer)]
    return _skewed_layer_call(
        _gla_kernel, "gla_layer", x, params,
        [
            pltpu.VMEM((GLA_VD, D_MODEL), _BF16),
            pltpu.VMEM((tm, D_MODEL), _BF16),
            pltpu.VMEM((tm, LANES), _BF16),
            pltpu.VMEM((2, tm, GLA_KD), _F32),
            pltpu.VMEM((2, tm, GLA_KD), _F32),
            pltpu.VMEM((2, tm, GLA_VD), _F32),
            pltpu.VMEM((2, GLA_VD, tm), _BF16),
            pltpu.VMEM((2, tm, GLA_KD), _F32),
            pltpu.VMEM((tm // GLA_BLOCK, 3, GLA_BLOCK, GLA_KD), _BF16),
            pltpu.VMEM((tm // GLA_BLOCK, GLA_HEADS, GLA_BLOCK, 2 * GLA_BLOCK), _BF16),
            pltpu.VMEM((tm, GLA_VD), _BF16),
            pltpu.VMEM((tm, D_MODEL), _F32),
            pltpu.VMEM((GLA_HEADS, GLA_DV, GLA_DK), _F32),
        ])


def _pool_layer(x, layer, ln_layer, w_in, w_grp, b_grp, scale, w_out, ln_g, ln_b):
    tm = TILE_M
    params = [(w_in, layer), (w_grp, layer), (b_grp, layer), (scale, layer), (w_out, layer),
              (ln_g, ln_layer), (ln_b, ln_layer)]
    return _skewed_layer_call(
        _pool_kernel, "pool_layer", x, params,
        [
            pltpu.VMEM((POOL_HALO, POOL_WIDTH), _F32),
            pltpu.VMEM((tm, POOL_WIDTH), _BF16),
            pltpu.VMEM((tm, D_MODEL), _F32),
        ])


def kernel(x, gla_w_in, gla_w_gk, gla_b_gk, gla_norm_w, gla_w_out, pool_w_in, pool_w_grp,
           pool_b_grp, pool_scale, pool_w_out, ln_g, ln_b):
    assert x.shape[1] % TILE_M == 0 and TILE_M % GLA_BLOCK == 0
    rank_pad = LANES - GLA_GATE_RANK
    gla = (gla_w_in.astype(_BF16),
           jnp.pad(gla_w_in[:, :, GLA_L0:], ((0, 0), (0, 0), (0, rank_pad))).astype(_BF16),
           jnp.pad(gla_w_gk, ((0, 0), (0, rank_pad), (0, 0))).astype(_BF16),
           gla_b_gk[:, None, :], gla_norm_w[:, None, :], gla_w_out.astype(_BF16))
    pool = (pool_w_in.astype(_BF16), pool_w_grp.astype(_BF16), pool_b_grp.reshape(-1, 1, POOL_WIDTH),
            pool_scale[:, None, :], pool_w_out.astype(_BF16))
    ln = (ln_g[:, None, :], ln_b[:, None, :])
    for i in range(DEPTH):
        layer = i // 2
        x = (_gla_layer if i % 2 == 0 else _pool_layer)(x, layer, i, *(gla if i % 2 == 0 else pool), *ln)
    return x
```
